```python
import jax, jax.numpy as jnp
from jax import lax
import numpy as np

D_MODEL = 1024
BATCH = 8
SEQ = 4096
DEPTH = 4

N_MIXERS = 2
N_REC_LAYERS = (DEPTH + 1) // 2
N_ATT_LAYERS = DEPTH // 2

LRU_WIDTH = D_MODEL
LRU_HEADS = 8
LRU_BLOCK = LRU_WIDTH // LRU_HEADS
CONV_WIDTH = 4
LRU_C = 8.0

N_HEADS = 8
HEAD_DIM = D_MODEL // N_HEADS
ROPE_THETA = 10000.0
MOBA_BLOCK = 256
MOBA_TOPK = 3
MOBA_Q_CHUNK = 32

N_EXPERTS = 32
TOP_K = 4
D_FF = D_MODEL
SWIGLU_ALPHA = 1.702
SWIGLU_LIMIT = 7.0
MOE_ROW_BLOCK = 256

DEEPNORM_ALPHA = (2.0 * DEPTH) ** 0.25
DEEPNORM_BETA = (8.0 * DEPTH) ** -0.25
LN_EPS = 1e-5

kernel_name = "hybrid_rglru_moba_moe_deepnorm"


def _layer_norm(x, g, b):
    xf = x.astype(jnp.float32)
    mu = jnp.mean(xf, axis=-1, keepdims=True)
    xc = xf - mu
    var = jnp.mean(xc * xc, axis=-1, keepdims=True)
    y = xc * lax.rsqrt(var + LN_EPS)
    return (y * g.astype(jnp.float32) + b.astype(jnp.float32)).astype(x.dtype)


def _rope(t):
    s = t.shape[2]
    half = HEAD_DIM // 2
    inv_freq = ROPE_THETA ** (-jnp.arange(half, dtype=jnp.float32) * 2.0 / HEAD_DIM)
    ang = jnp.arange(s, dtype=jnp.float32)[:, None] * inv_freq[None, :]
    cos, sin = jnp.cos(ang), jnp.sin(ang)
    tf = t.astype(jnp.float32)
    t1, t2 = tf[..., :half], tf[..., half:]
    return jnp.concatenate([t1 * cos - t2 * sin, t2 * cos + t1 * sin], axis=-1).astype(t.dtype)


def _rg_lru_mixer(x, w_in, conv_w, conv_b, w_ga, b_ga, w_gx, b_gx, lam, w_out):
    b, s, _ = x.shape
    proj = x @ w_in
    gate_branch, rec = proj[..., :LRU_WIDTH], proj[..., LRU_WIDTH:]
    u = lax.conv_general_dilated(rec, conv_w[:, None, :], window_strides=(1,), padding=[(CONV_WIDTH - 1, 0)],
                                 dimension_numbers=('NWC', 'WIO', 'NWC'), feature_group_count=LRU_WIDTH) + conv_b
    ub = u.reshape(b, s, LRU_HEADS, LRU_BLOCK)
    gate_a = jax.nn.sigmoid(jnp.einsum('bshi,hij->bshj', ub, w_ga).reshape(b, s, LRU_WIDTH) + b_ga)
    gate_x = jax.nn.sigmoid(jnp.einsum('bshi,hij->bshj', ub, w_gx).reshape(b, s, LRU_WIDTH) + b_gx)
    log_a = LRU_C * gate_a.astype(jnp.float32) * jax.nn.log_sigmoid(lam.astype(jnp.float32))
    a = jnp.exp(log_a)
    mult = jnp.sqrt(-jnp.expm1(2.0 * log_a))
    mult = jnp.where(jnp.arange(s)[None, :, None] == 0, 1.0, mult)
    b_in = mult * (gate_x * u).astype(jnp.float32)

    def combine(left, right):
        a_l, b_l = left
        a_r, b_r = right
        return a_l * a_r, a_r * b_l + b_r

    _, h = lax.associative_scan(combine, (a, b_in), axis=1)
    y = h.astype(x.dtype) * jax.nn.gelu(gate_branch, approximate=True)
    return y @ w_out


def _moba_mixer(x, w_qkv, w_o):
    b, s, _ = x.shape
    qkv = (x @ w_qkv).reshape(b, s, 3, N_HEADS, HEAD_DIM)
    q = _rope(jnp.transpose(qkv[:, :, 0], (0, 2, 1, 3)))
    k = _rope(jnp.transpose(qkv[:, :, 1], (0, 2, 1, 3)))
    v = jnp.transpose(qkv[:, :, 2], (0, 2, 1, 3))
    nb = -(-s // MOBA_BLOCK)
    pad = nb * MOBA_BLOCK - s
    kp = jnp.pad(k, ((0, 0), (0, 0), (0, pad), (0, 0)))
    vp = jnp.pad(v, ((0, 0), (0, 0), (0, pad), (0, 0)))
    kblk = kp.reshape(b, N_HEADS, nb, MOBA_BLOCK, HEAD_DIM)
    vblk = vp.reshape(b, N_HEADS, nb, MOBA_BLOCK, HEAD_DIM)
    kmean = jnp.mean(kblk.astype(jnp.float32), axis=3)
    topk = min(MOBA_TOPK, nb)
    n_chunks = s // MOBA_Q_CHUNK
    scale = HEAD_DIM ** -0.5
    bi = jnp.arange(b)[:, None, None, None]
    hi = jnp.arange(N_HEADS)[None, :, None, None]

    def chunk_fn(ci):
        q0 = ci * MOBA_Q_CHUNK
        qc = lax.dynamic_slice_in_dim(q, q0, MOBA_Q_CHUNK, axis=2)
        qblk = q0 // MOBA_BLOCK
        qpos = q0 + jnp.arange(MOBA_Q_CHUNK)
        s_blk = jnp.einsum('bhqd,bhnd->bhqn', qc.astype(jnp.float32), kmean)
        s_blk = jnp.where(jnp.arange(nb) < qblk, s_blk, -jnp.inf)
        _, sel = lax.top_k(s_blk, topk)
        valid = jnp.arange(topk) < qblk
        k_sel = kblk[bi, hi, sel]
        v_sel = vblk[bi, hi, sel]
        s_sel = jnp.einsum('bhqd,bhqnjd->bhqnj', qc, k_sel).astype(jnp.float32) * scale
        s_sel = jnp.where(valid[:, None], s_sel, -jnp.inf).reshape(b, N_HEADS, MOBA_Q_CHUNK, topk * MOBA_BLOCK)
        k_own = lax.dynamic_slice_in_dim(kp, qblk * MOBA_BLOCK, MOBA_BLOCK, axis=2)
        v_own = lax.dynamic_slice_in_dim(vp, qblk * MOBA_BLOCK, MOBA_BLOCK, axis=2)
        s_own = jnp.einsum('bhqd,bhjd->bhqj', qc, k_own).astype(jnp.float32) * scale
        kpos = qblk * MOBA_BLOCK + jnp.arange(MOBA_BLOCK)
        s_own = jnp.where(kpos[None, :] <= qpos[:, None], s_own, -jnp.inf)
        p = jax.nn.softmax(jnp.concatenate([s_sel, s_own], axis=-1), axis=-1)
        p_sel = p[..., :topk * MOBA_BLOCK].reshape(b, N_HEADS, MOBA_Q_CHUNK, topk, MOBA_BLOCK).astype(v.dtype)
        p_own = p[..., topk * MOBA_BLOCK:].astype(v.dtype)
        return (jnp.einsum('bhqnj,bhqnjd->bhqd', p_sel, v_sel)
                + jnp.einsum('bhqj,bhjd->bhqd', p_own, v_own))

    o = lax.map(chunk_fn, jnp.arange(n_chunks))
    o = jnp.transpose(o, (1, 0, 3, 2, 4)).reshape(b, s, N_HEADS * HEAD_DIM)
    return o @ w_o


def _moe(x, w_router, b_router, w_gu, b_gu, w_down, b_down):
    b, s, d = x.shape
    t = b * s
    x2 = x.reshape(t, d)
    logits = (x2 @ w_router + b_router).astype(jnp.float32)
    top_val, top_idx = lax.top_k(logits, TOP_K)
    gates = jax.nn.softmax(top_val, axis=-1)
    tk = t * TOP_K
    flat_e = top_idx.reshape(tk)
    flat_tok = jnp.arange(tk) // TOP_K
    flat_g = gates.reshape(tk)
    order = jnp.argsort(flat_e)
    se = flat_e[order]
    counts = jnp.bincount(flat_e, length=N_EXPERTS)
    starts = jnp.cumsum(counts) - counts
    pcounts = ((counts + MOE_ROW_BLOCK - 1) // MOE_ROW_BLOCK) * MOE_ROW_BLOCK
    pends = jnp.cumsum(pcounts)
    pstarts = pends - pcounts
    dest = pstarts[se] + jnp.arange(tk) - starts[se]
    n_blocks = (tk + N_EXPERTS * (MOE_ROW_BLOCK - 1) + MOE_ROW_BLOCK - 1) // MOE_ROW_BLOCK
    p_rows = n_blocks * MOE_ROW_BLOCK
    row_tok = jnp.full((p_rows,), t, jnp.int32).at[dest].set(flat_tok[order])
    row_gate = jnp.zeros((p_rows,), jnp.float32).at[dest].set(flat_g[order])
    blk_expert = jnp.minimum(jnp.searchsorted(pends, jnp.arange(n_blocks) * MOE_ROW_BLOCK, side='right'), N_EXPERTS - 1)
    x_pad = jnp.concatenate([x2, jnp.zeros((1, d), x2.dtype)], axis=0)
    xb = x_pad[row_tok].reshape(n_blocks, MOE_ROW_BLOCK, d)

    def expert_block(args):
        xb_i, e = args
        gu = xb_i @ w_gu[e] + b_gu[e]
        g = jnp.minimum(gu[:, :D_FF], SWIGLU_LIMIT)
        u = jnp.clip(gu[:, D_FF:], -SWIGLU_LIMIT, SWIGLU_LIMIT)
        h = (u + 1.0) * (g * jax.nn.sigmoid(SWIGLU_ALPHA * g))
        return h @ w_down[e] + b_down[e]

    yb = lax.map(expert_block, (xb, blk_expert))
    y = yb.reshape(p_rows, d).astype(jnp.float32) * row_gate[:, None]
    out = jax.ops.segment_sum(y, row_tok, num_segments=t + 1)[:t]
    return out.astype(x.dtype).reshape(b, s, d)


def setup_inputs(seed: int = 0) -> dict:
    key = jax.random.key(seed)
    ks = jax.random.split(key, 24)
    f32 = jnp.float32

    def nrm(k, shape, sc):
        return jax.random.normal(k, shape, f32) * sc

    d, w = D_MODEL, LRU_WIDTH
    x = nrm(ks[0], (BATCH, SEQ, d), 1.0)
    ln1_g = 1.0 + nrm(ks[1], (DEPTH, d), 0.02)
    ln1_b = nrm(ks[2], (DEPTH, d), 0.02)
    ln2_g = 1.0 + nrm(ks[3], (DEPTH, d), 0.02)
    ln2_b = nrm(ks[4], (DEPTH, d), 0.02)
    w_in_rec = nrm(ks[5], (N_REC_LAYERS, d, 2 * w), d ** -0.5)
    conv_w = nrm(ks[6], (N_REC_LAYERS, CONV_WIDTH, w), CONV_WIDTH ** -0.5)
    conv_b = nrm(ks[7], (N_REC_LAYERS, w), 0.01)
    w_gate_a = nrm(ks[8], (N_REC_LAYERS, LRU_HEADS, LRU_BLOCK, LRU_BLOCK), LRU_BLOCK ** -0.5)
    b_gate_a = nrm(ks[9], (N_REC_LAYERS, w), 0.01)
    w_gate_x = nrm(ks[10], (N_REC_LAYERS, LRU_HEADS, LRU_BLOCK, LRU_BLOCK), LRU_BLOCK ** -0.5)
    b_gate_x = nrm(ks[11], (N_REC_LAYERS, w), 0.01)
    a0 = jax.random.uniform(ks[12], (N_REC_LAYERS, w), f32, 0.9, 0.999)
    lru_lambda = jnp.log(a0) - jnp.log1p(-a0)
    w_out_rec = nrm(ks[13], (N_REC_LAYERS, w, d), (w ** -0.5) * DEEPNORM_BETA)
    w_qk = nrm(ks[14], (N_ATT_LAYERS, d, 2 * d), d ** -0.5)
    w_v = nrm(ks[15], (N_ATT_LAYERS, d, d), (d ** -0.5) * DEEPNORM_BETA)
    w_qkv = jnp.concatenate([w_qk, w_v], axis=-1)
    w_o = nrm(ks[16], (N_ATT_LAYERS, d, d), (d ** -0.5) * DEEPNORM_BETA)
    w_router = nrm(ks[17], (DEPTH, d, N_EXPERTS), d ** -0.5)
    b_router = nrm(ks[18], (DEPTH, N_EXPERTS), 0.01)
    w_gu = nrm(ks[19], (DEPTH, N_EXPERTS, d, 2 * D_FF), d ** -0.5)
    b_gu = nrm(ks[20], (DEPTH, N_EXPERTS, 2 * D_FF), 0.01)
    w_down = nrm(ks[21], (DEPTH, N_EXPERTS, D_FF, d), (D_FF ** -0.5) * DEEPNORM_BETA)
    b_down = nrm(ks[22], (DEPTH, N_EXPERTS, d), 0.01)
    return {"x": x, "ln1_g": ln1_g, "ln1_b": ln1_b, "ln2_g": ln2_g, "ln2_b": ln2_b,
            "w_in_rec": w_in_rec, "conv_w": conv_w, "conv_b": conv_b,
            "w_gate_a": w_gate_a, "b_gate_a": b_gate_a, "w_gate_x": w_gate_x, "b_gate_x": b_gate_x,
            "lru_lambda": lru_lambda, "w_out_rec": w_out_rec, "w_qkv": w_qkv, "w_o": w_o,
            "w_router": w_router, "b_router": b_router, "w_gu": w_gu, "b_gu": b_gu,
            "w_down": w_down, "b_down": b_down}


def reference(x, ln1_g, ln1_b, ln2_g, ln2_b, w_in_rec, conv_w, conv_b, w_gate_a, b_gate_a, w_gate_x, b_gate_x,
              lru_lambda, w_out_rec, w_qkv, w_o, w_router, b_router, w_gu, b_gu, w_down, b_down):
    for i in range(DEPTH):
        j = i // N_MIXERS
        if i % N_MIXERS == 0:
            y = _rg_lru_mixer(x, w_in_rec[j], conv_w[j], conv_b[j], w_gate_a[j], b_gate_a[j],
                              w_gate_x[j], b_gate_x[j], lru_lambda[j], w_out_rec[j])
        else:
            y = _moba_mixer(x, w_qkv[j], w_o[j])
        x = _layer_norm(DEEPNORM_ALPHA * x + y, ln1_g[i], ln1_b[i])
        y = _moe(x, w_router[i], b_router[i], w_gu[i], b_gu[i], w_down[i], b_down[i])
        x = _layer_norm(DEEPNORM_ALPHA * x + y, ln2_g[i], ln2_b[i])
    return x
```

```python
import functools
import math

import jax
import jax.numpy as jnp
from jax import lax
from jax.experimental import pallas as pl
from jax.experimental.pallas import tpu as pltpu

DEPTH = 4
N_HEADS = 8
LRU_HEADS = 8
CONV_WIDTH = 4
LRU_C = 8.0
ROPE_THETA = 10000.0
MOBA_BLOCK = 256
MOBA_TOPK = 3
N_EXPERTS = 32
TOP_K = 4
SWIGLU_ALPHA = 1.702
SWIGLU_LIMIT = 7.0
DEEPNORM_ALPHA = (2.0 * DEPTH) ** 0.25
LN_EPS = 1e-5

SUBLANES = 8
LANES = 128
VMEM_LIMIT_BYTES = 56 * 1024 * 1024

ROW_TILE = 512
LRU_CHUNK = 64
MOE_ROW_TILE = 256
COMBINE_TILE = 256

_NT_DIMS = (((1,), (1,)), ((), ()))


def _params(*sem):
    return pltpu.CompilerParams(dimension_semantics=sem, vmem_limit_bytes=VMEM_LIMIT_BYTES)


def _gelu_tanh(x):
    return 0.5 * x * (1.0 + jnp.tanh(math.sqrt(2.0 / math.pi) * (x + 0.044715 * (x * x * x))))


def _layer_norm_rows(z, g, b):
    mu = jnp.mean(z, axis=-1, keepdims=True)
    zc = z - mu
    var = jnp.mean(zc * zc, axis=-1, keepdims=True)
    return zc * lax.rsqrt(var + LN_EPS) * g + b


def _lru_in_proj_kernel(x_ref, w_ref, o_ref):
    w = o_ref.shape[1] // 2
    acc = jnp.dot(x_ref[...].astype(jnp.bfloat16), w_ref[...], preferred_element_type=jnp.float32)
    o_ref[:, :w] = _gelu_tanh(acc[:, :w])
    o_ref[:, w:] = acc[:, w:]


def _lru_in_proj(x2, w_in_bf16, batch, seq):
    d = x2.shape[1]
    w2 = w_in_bf16.shape[1]
    ns = seq // ROW_TILE
    return pl.pallas_call(
        _lru_in_proj_kernel,
        grid=(batch, ns),
        in_specs=[pl.BlockSpec((ROW_TILE, d), lambda b, i: (b * ns + i, 0)),
                  pl.BlockSpec((d, w2), lambda b, i: (0, 0))],
        out_specs=pl.BlockSpec((ROW_TILE, w2), lambda b, i: (i, b)),
        out_shape=jax.ShapeDtypeStruct((seq, batch * w2), jnp.float32),
        compiler_params=_params("arbitrary", "arbitrary"),
        name="lru_in_proj",
    )(x2, w_in_bf16)


def _lru_kernel(gate_ref, rec_ref, cw_ref, cb_ref, wg_ref, bga_ref, bgx_ref, lam_ref, y_ref,
                ext_ref, a_ref, b_ref, h_ref):
    i = pl.program_id(0)
    n, w = rec_ref.shape
    halo = (CONV_WIDTH - 1) * SUBLANES
    hb = w // LRU_HEADS
    sub = 128

    @pl.when(i == 0)
    def _():
        ext_ref[0:halo, :] = jnp.zeros((halo, w), jnp.float32)
        h_ref[...] = jnp.zeros_like(h_ref)

    ext_ref[halo:halo + n, :] = rec_ref[...]
    c_log = LRU_C * jax.nn.log_sigmoid(lam_ref[...])

    def sub_block(r, carry):
        r0 = pl.multiple_of(r * sub, sub)
        for hd in range(LRU_HEADS):
            cs = slice(hd * hb, (hd + 1) * hb)
            u = cb_ref[:, cs]
            for j in range(CONV_WIDTH):
                u = u + cw_ref[j:j + 1, cs] * ext_ref[pl.ds(r0 + j * SUBLANES, sub), cs]
            gates = jnp.dot(u.astype(jnp.bfloat16), wg_ref[hd], preferred_element_type=jnp.float32)
            gate_a = jax.nn.sigmoid(gates[:, :hb] + bga_ref[:, cs])
            gate_x = jax.nn.sigmoid(gates[:, hb:] + bgx_ref[:, cs])
            log_a = gate_a * c_log[:, cs]
            a = jnp.exp(log_a)
            mult = jnp.sqrt(1.0 - a * a)
            row = i * n + r0 + lax.broadcasted_iota(jnp.int32, (sub, hb), 0)
            mult = jnp.where(row < SUBLANES, 1.0, mult)
            a_ref[pl.ds(r0, sub), cs] = a
            b_ref[pl.ds(r0, sub), cs] = mult * (gate_x * u)
        return carry

    lax.fori_loop(0, n // sub, sub_block, 0)
    ext_ref[0:halo, :] = ext_ref[n:n + halo, :]

    def step(t, h):
        r0 = pl.multiple_of(t * SUBLANES, SUBLANES)
        h = a_ref[pl.ds(r0, SUBLANES), :] * h + b_ref[pl.ds(r0, SUBLANES), :]
        b_ref[pl.ds(r0, SUBLANES), :] = h
        return h

    h_ref[...] = lax.fori_loop(0, n // SUBLANES, step, h_ref[...], unroll=4)
    y_ref[...] = (b_ref[...] * gate_ref[...]).astype(y_ref.dtype)


def _lru(proj_tm, conv_w, conv_b, wg_bf16, b_ga, b_gx, lam, seq, batch):
    assert batch == SUBLANES
    w = conv_w.shape[1]
    rows = seq * batch
    n = LRU_CHUNK * batch
    halo = (CONV_WIDTH - 1) * SUBLANES
    proj2 = proj_tm.reshape(rows, 2 * w)
    vec = lambda: pl.BlockSpec((1, w), lambda i: (0, 0))
    return pl.pallas_call(
        _lru_kernel,
        grid=(rows // n,),
        in_specs=[pl.BlockSpec((n, w), lambda i: (i, 0)),
                  pl.BlockSpec((n, w), lambda i: (i, 1)),
                  pl.BlockSpec((CONV_WIDTH, w), lambda i: (0, 0)),
                  vec(),
                  pl.BlockSpec(wg_bf16.shape, lambda i: (0, 0, 0)),
                  vec(), vec(), vec()],
        out_specs=pl.BlockSpec((n, w), lambda i: (i, 0)),
        out_shape=jax.ShapeDtypeStruct((rows, w), jnp.bfloat16),
        scratch_shapes=[pltpu.VMEM((n + halo, w), jnp.float32),
                        pltpu.VMEM((n, w), jnp.float32),
                        pltpu.VMEM((n, w), jnp.float32),
                        pltpu.VMEM((SUBLANES, w), jnp.float32)],
        compiler_params=_params("arbitrary"),
        name="lru_scan",
    )(proj2, proj2, conv_w, conv_b.reshape(1, w), wg_bf16, b_ga.reshape(1, w), b_gx.reshape(1, w),
      lam.reshape(1, w))


def _post_mixer_kernel(y_ref, x_ref, w_ref, g_ref, b_ref, wr_ref, br_ref, xn_ref, idx_ref, gate_ref):
    acc = jnp.dot(y_ref[...], w_ref[...], preferred_element_type=jnp.float32)
    xn = _layer_norm_rows(DEEPNORM_ALPHA * x_ref[...] + acc, g_ref[...], b_ref[...])
    xn_ref[...] = xn
    logits = lax.dot_general(wr_ref[...], xn, _NT_DIMS, precision=lax.Precision.HIGHEST,
                             preferred_element_type=jnp.float32) + br_ref[...]
    e_iota = lax.broadcasted_iota(jnp.int32, logits.shape, 0)
    vals, idxs = [], []
    for _ in range(TOP_K):
        m = jnp.max(logits, axis=0, keepdims=True)
        sel = jnp.min(jnp.where(logits == m, e_iota, N_EXPERTS), axis=0, keepdims=True)
        logits = jnp.where(e_iota == sel, -jnp.inf, logits)
        vals.append(m)
        idxs.append(sel)
    exps = [jnp.exp(v - vals[0]) for v in vals]
    denom = exps[0] + exps[1] + exps[2] + exps[3]
    for k in range(TOP_K):
        idx_ref[k:k + 1, :] = idxs[k]
        gate_ref[k:k + 1, :] = exps[k] / denom


def _post_mixer(y, y_spec, x2, w_out_bf16, ln_g, ln_b, w_router, b_router, batch, seq):
    t, d = x2.shape
    ns = seq // ROW_TILE
    row_map = lambda b, i: (b * ns + i, 0)
    lane_map = lambda b, i: (0, b * ns + i)
    const = lambda b, i: (0, 0)
    return pl.pallas_call(
        _post_mixer_kernel,
        grid=(batch, ns),
        in_specs=[y_spec,
                  pl.BlockSpec((ROW_TILE, d), row_map),
                  pl.BlockSpec(w_out_bf16.shape, const),
                  pl.BlockSpec((1, d), const),
                  pl.BlockSpec((1, d), const),
                  pl.BlockSpec((N_EXPERTS, d), const),
                  pl.BlockSpec((N_EXPERTS, 1), const)],
        out_specs=[pl.BlockSpec((ROW_TILE, d), row_map),
                   pl.BlockSpec((TOP_K, ROW_TILE), lane_map),
                   pl.BlockSpec((TOP_K, ROW_TILE), lane_map)],
        out_shape=[jax.ShapeDtypeStruct((t, d), jnp.float32),
                   jax.ShapeDtypeStruct((TOP_K, t), jnp.int32),
                   jax.ShapeDtypeStruct((TOP_K, t), jnp.float32)],
        compiler_params=_params("arbitrary", "arbitrary"),
        name="post_mixer",
    )(y, x2, w_out_bf16, ln_g.reshape(1, d), ln_b.reshape(1, d), w_router.T, b_router.reshape(N_EXPERTS, 1))


def _qkv_kernel(x_ref, w_ref, cos_ref, sin_ref, o_ref):
    hd = o_ref.shape[3]
    acc = jnp.dot(x_ref[...].astype(jnp.bfloat16), w_ref[...], preferred_element_type=jnp.float32)
    cos2 = cos_ref[...]
    sin2 = sin_ref[...]
    for j in range(3 * N_HEADS):
        t = acc[:, j * hd:(j + 1) * hd]
        if j < 2 * N_HEADS:
            t = t * cos2 + pltpu.roll(t, hd // 2, 1) * sin2
        o_ref[0, j] = t


def _qkv_proj(x2, w_qkv_bf16, cos2, sin2, batch, seq):
    t, d = x2.shape
    hd = d // N_HEADS
    ns = seq // ROW_TILE
    return pl.pallas_call(
        _qkv_kernel,
        grid=(batch, ns),
        in_specs=[pl.BlockSpec((ROW_TILE, d), lambda b, i: (b * ns + i, 0)),
                  pl.BlockSpec(w_qkv_bf16.shape, lambda b, i: (0, 0)),
                  pl.BlockSpec((ROW_TILE, hd), lambda b, i: (i, 0)),
                  pl.BlockSpec((ROW_TILE, hd), lambda b, i: (i, 0))],
        out_specs=pl.BlockSpec((1, 3 * N_HEADS, ROW_TILE, hd), lambda b, i: (b, 0, i, 0)),
        out_shape=jax.ShapeDtypeStruct((batch, 3 * N_HEADS, seq, hd), jnp.float32),
        compiler_params=_params("arbitrary", "arbitrary"),
        name="qkv_proj",
    )(x2, w_qkv_bf16, cos2, sin2)


def _moba_kernel(q_ref, k_ref, v_ref, o_ref, kmean_ref):
    qi = pl.program_id(2)
    blk = MOBA_BLOCK
    seq, hd = k_ref.shape[2], k_ref.shape[3]
    nb = seq // blk
    scale = hd ** -0.5

    @pl.when(qi == 0)
    def _():
        for n in range(nb):
            kmean_ref[n:n + 1, :] = jnp.mean(k_ref[0, 0, n * blk:(n + 1) * blk, :], axis=0, keepdims=True)

    q = q_ref[0, 0]
    qb = q.astype(jnp.bfloat16)

    sg = lax.dot_general(q, kmean_ref[...], _NT_DIMS, precision=lax.Precision.HIGHEST,
                         preferred_element_type=jnp.float32)
    n_iota = lax.broadcasted_iota(jnp.int32, sg.shape, 1)
    sg = jnp.where(n_iota < qi, sg, -jnp.inf)
    chosen = jnp.zeros(sg.shape, jnp.float32)
    for j in range(MOBA_TOPK):
        m = jnp.max(sg, axis=1, keepdims=True)
        first = jnp.min(jnp.where(sg == m, n_iota, nb), axis=1, keepdims=True)
        hit = (n_iota == first) & (j < qi)
        chosen = jnp.where(hit, 1.0, chosen)
        sg = jnp.where(n_iota == first, -jnp.inf, sg)

    r0 = pl.multiple_of(qi * blk, blk)
    k_own = k_ref[0, 0, pl.ds(r0, blk), :].astype(jnp.bfloat16)
    v_own = v_ref[0, 0, pl.ds(r0, blk), :].astype(jnp.bfloat16)
    s = lax.dot_general(qb, k_own, _NT_DIMS, preferred_element_type=jnp.float32) * scale
    causal = lax.broadcasted_iota(jnp.int32, s.shape, 1) <= lax.broadcasted_iota(jnp.int32, s.shape, 0)
    s = jnp.where(causal, s, -jnp.inf)
    m0 = jnp.max(s, axis=1, keepdims=True)
    p = jnp.exp(s - m0)
    l0 = jnp.sum(p, axis=1, keepdims=True)
    acc0 = jnp.dot(p.astype(jnp.bfloat16), v_own, preferred_element_type=jnp.float32)

    def past_block(n, carry):
        m_run, l_run, acc = carry
        c0 = pl.multiple_of(n * blk, blk)
        k_n = k_ref[0, 0, pl.ds(c0, blk), :].astype(jnp.bfloat16)
        v_n = v_ref[0, 0, pl.ds(c0, blk), :].astype(jnp.bfloat16)
        picked = jnp.sum(jnp.where(n_iota == n, chosen, 0.0), axis=1, keepdims=True) > 0.5
        s_n = lax.dot_general(qb, k_n, _NT_DIMS, preferred_element_type=jnp.float32) * scale
        s_n = jnp.where(picked, s_n, -jnp.inf)
        m_new = jnp.maximum(m_run, jnp.max(s_n, axis=1, keepdims=True))
        p_n = jnp.exp(s_n - m_new)
        corr = jnp.exp(m_run - m_new)
        l_new = corr * l_run + jnp.sum(p_n, axis=1, keepdims=True)
        acc = corr * acc + jnp.dot(p_n.astype(jnp.bfloat16), v_n, preferred_element_type=jnp.float32)
        return m_new, l_new, acc

    _, l_fin, acc = lax.fori_loop(0, qi, past_block, (m0, l0, acc0))
    o_ref[0] = (acc / l_fin).astype(o_ref.dtype)


def _moba(qkv, batch, seq, d):
    hd = d // N_HEADS
    nb = seq // MOBA_BLOCK
    return pl.pallas_call(
        _moba_kernel,
        grid=(batch, N_HEADS, nb),
        in_specs=[pl.BlockSpec((1, 1, MOBA_BLOCK, hd), lambda b, h, i: (b, h, i, 0)),
                  pl.BlockSpec((1, 1, seq, hd), lambda b, h, i: (b, N_HEADS + h, 0, 0)),
                  pl.BlockSpec((1, 1, seq, hd), lambda b, h, i: (b, 2 * N_HEADS + h, 0, 0))],
        out_specs=pl.BlockSpec((1, MOBA_BLOCK, hd), lambda b, h, i: (b, i, h)),
        out_shape=jax.ShapeDtypeStruct((batch, seq, d), jnp.bfloat16),
        scratch_shapes=[pltpu.VMEM((nb, hd), jnp.float32)],
        compiler_params=_params("arbitrary", "arbitrary", "arbitrary"),
        name="moba_attn",
    )(qkv, qkv, qkv)


def _moe_num_blocks(t):
    return (t * TOP_K + N_EXPERTS * (MOE_ROW_TILE - 1) + MOE_ROW_TILE - 1) // MOE_ROW_TILE


def _routing_tables(idx, t):
    n_blocks = _moe_num_blocks(t)
    p_rows = n_blocks * MOE_ROW_TILE
    tk = t * TOP_K
    flat_e = idx.reshape(tk)
    order = jnp.argsort(flat_e, stable=True).astype(jnp.int32)
    se = flat_e[order]
    e_ids = jnp.arange(N_EXPERTS, dtype=jnp.int32)
    starts = jnp.searchsorted(se, e_ids, side='left').astype(jnp.int32)
    counts = jnp.searchsorted(se, e_ids, side='right').astype(jnp.int32) - starts
    pcounts = ((counts + MOE_ROW_TILE - 1) // MOE_ROW_TILE) * MOE_ROW_TILE
    pends = jnp.cumsum(pcounts)
    pstarts = pends - pcounts
    blk_expert = jnp.minimum(
        jnp.searchsorted(pends, jnp.arange(n_blocks, dtype=jnp.int32) * MOE_ROW_TILE, side='right'),
        N_EXPERTS - 1).astype(jnp.int32)
    row_e = jnp.repeat(blk_expert, MOE_ROW_TILE)
    j = jnp.arange(p_rows, dtype=jnp.int32) - pstarts[row_e]
    real = (j >= 0) & (j < counts[row_e])
    pair = order[jnp.clip(starts[row_e] + j, 0, tk - 1)]
    tok = pair % t
    kk = pair // t
    pad = 1 - real.astype(jnp.int32)
    pad_rank = jnp.cumsum(pad) - pad
    row_tok = jnp.where(real, tok, 0).astype(jnp.int32)
    row_dst = jnp.where(real, tok * TOP_K + kk, tk + pad_rank).astype(jnp.int32)
    tables = jnp.zeros((n_blocks, SUBLANES, MOE_ROW_TILE), jnp.int32)
    tables = tables.at[:, 0, :].set(row_tok.reshape(n_blocks, MOE_ROW_TILE))
    tables = tables.at[:, 1, :].set(row_dst.reshape(n_blocks, MOE_ROW_TILE))
    return blk_expert, tables, p_rows - tk


def _moe_kernel(blk_e_ref, tab_hbm, x_hbm, wgu_ref, bgu_ref, wd_ref, bd_ref, y_hbm,
                tab_smem, xbuf, ybuf, sem_tab, sem_in, sem_out):
    i = pl.program_id(0)
    nb = pl.num_programs(0)
    rows, d = xbuf.shape[1], xbuf.shape[2]
    dff = wd_ref.shape[1]
    slot = i % 2

    def tab_copy(blk):
        return pltpu.make_async_copy(tab_hbm.at[blk], tab_smem.at[blk % 3], sem_tab.at[blk % 3])

    def start_gather(blk):
        ts, xs = blk % 3, blk % 2

        def body(r, c):
            tok = tab_smem[ts, 0, r]
            pltpu.make_async_copy(x_hbm.at[pl.ds(tok, 1)], xbuf.at[xs, pl.ds(r, 1)], sem_in.at[xs]).start()
            return c

        lax.fori_loop(0, rows, body, 0, unroll=8)

    def wait_gather(xs):
        pltpu.make_async_copy(x_hbm.at[pl.ds(0, rows)], xbuf.at[xs], sem_in.at[xs]).wait()

    def start_scatter(blk):
        ts, ys = blk % 3, blk % 2

        def body(r, c):
            dst = tab_smem[ts, 1, r]
            pltpu.make_async_copy(ybuf.at[ys, pl.ds(r, 1)], y_hbm.at[pl.ds(dst, 1)], sem_out.at[ys]).start()
            return c

        lax.fori_loop(0, rows, body, 0, unroll=8)

    def wait_scatter(ys):
        pltpu.make_async_copy(ybuf.at[ys], y_hbm.at[pl.ds(0, rows)], sem_out.at[ys]).wait()

    @pl.when(i == 0)
    def _():
        tab_copy(0).start()
        tab_copy(0).wait()
        start_gather(0)

        @pl.when(nb > 1)
        def _():
            tab_copy(1).start()

    @pl.when(i + 1 < nb)
    def _():
        tab_copy(i + 1).wait()

        @pl.when(i + 2 < nb)
        def _():
            tab_copy(i + 2).start()

        start_gather(i + 1)

    wait_gather(slot)

    @pl.when(i >= 2)
    def _():
        wait_scatter(slot)

    x = xbuf[slot].astype(jnp.bfloat16)
    gu = jnp.dot(x, wgu_ref[0], preferred_element_type=jnp.float32) + bgu_ref[0]
    g = jnp.minimum(gu[:, :dff], SWIGLU_LIMIT)
    u = jnp.clip(gu[:, dff:], -SWIGLU_LIMIT, SWIGLU_LIMIT)
    h = (u + 1.0) * (g * jax.nn.sigmoid(SWIGLU_ALPHA * g))
    ybuf[slot] = jnp.dot(h.astype(jnp.bfloat16), wd_ref[0], preferred_element_type=jnp.float32) + bd_ref[0]
    start_scatter(i)

    @pl.when(i == nb - 1)
    def _():
        @pl.when(nb > 1)
        def _():
            wait_scatter(1 - slot)

        wait_scatter(slot)


def _moe_experts(xn, blk_expert, tables, n_dump, w_gu_bf16, b_gu, w_down_bf16, b_down):
    t, d = xn.shape
    n_blocks = tables.shape[0]
    dff = w_down_bf16.shape[1]
    out_rows = t * TOP_K + n_dump
    grid_spec = pltpu.PrefetchScalarGridSpec(
        num_scalar_prefetch=1,
        grid=(n_blocks,),
        in_specs=[pl.BlockSpec(memory_space=pl.ANY),
                  pl.BlockSpec(memory_space=pl.ANY),
                  pl.BlockSpec((1, d, 2 * dff), lambda i, be: (be[i], 0, 0)),
                  pl.BlockSpec((1, 1, 2 * dff), lambda i, be: (be[i], 0, 0)),
                  pl.BlockSpec((1, dff, d), lambda i, be: (be[i], 0, 0)),
                  pl.BlockSpec((1, 1, d), lambda i, be: (be[i], 0, 0))],
        out_specs=pl.BlockSpec(memory_space=pl.ANY),
        scratch_shapes=[pltpu.SMEM((3, SUBLANES, MOE_ROW_TILE), jnp.int32),
                        pltpu.VMEM((2, MOE_ROW_TILE, d), jnp.float32),
                        pltpu.VMEM((2, MOE_ROW_TILE, d), jnp.float32),
                        pltpu.SemaphoreType.DMA((3,)),
                        pltpu.SemaphoreType.DMA((2,)),
                        pltpu.SemaphoreType.DMA((2,))])
    return pl.pallas_call(
        _moe_kernel,
        grid_spec=grid_spec,
        out_shape=jax.ShapeDtypeStruct((out_rows, d), jnp.float32),
        compiler_params=_params("arbitrary"),
        name="moe_experts",
    )(blk_expert, tables, xn, w_gu_bf16, b_gu.reshape(N_EXPERTS, 1, 2 * dff), w_down_bf16,
      b_down.reshape(N_EXPERTS, 1, d))


def _combine_kernel(y_ref, gate_ref, x_ref, g_ref, b_ref, o_ref):
    d = x_ref.shape[1]
    moe = gate_ref[:, 0:1] * y_ref[:, 0:d]
    for k in range(1, TOP_K):
        moe = moe + gate_ref[:, k:k + 1] * y_ref[:, k * d:(k + 1) * d]
    o_ref[...] = _layer_norm_rows(DEEPNORM_ALPHA * x_ref[...] + moe, g_ref[...], b_ref[...])


def _combine(y_scat, gates_t, xn, ln_g, ln_b):
    t, d = xn.shape
    y4 = y_scat.reshape(y_scat.shape[0] // TOP_K, TOP_K * d)
    const = lambda i: (0, 0)
    row = lambda i: (i, 0)
    return pl.pallas_call(
        _combine_kernel,
        grid=(t // COMBINE_TILE,),
        in_specs=[pl.BlockSpec((COMBINE_TILE, TOP_K * d), row),
                  pl.BlockSpec((COMBINE_TILE, TOP_K), row),
                  pl.BlockSpec((COMBINE_TILE, d), row),
                  pl.BlockSpec((1, d), const),
                  pl.BlockSpec((1, d), const)],
        out_specs=pl.BlockSpec((COMBINE_TILE, d), row),
        out_shape=jax.ShapeDtypeStruct((t, d), jnp.float32),
        compiler_params=_params("arbitrary"),
        name="moe_combine",
    )(y4, gates_t, xn, ln_g.reshape(1, d), ln_b.reshape(1, d))


def _rope_tables(seq, hd):
    half = hd // 2
    inv_freq = ROPE_THETA ** (-jnp.arange(half, dtype=jnp.float32) * 2.0 / hd)
    ang = jnp.arange(seq, dtype=jnp.float32)[:, None] * inv_freq[None, :]
    cos, sin = jnp.cos(ang), jnp.sin(ang)
    return jnp.concatenate([cos, cos], axis=-1), jnp.concatenate([-sin, sin], axis=-1)


def kernel(x, ln1_g, ln1_b, ln2_g, ln2_b, w_in_rec, conv_w, conv_b, w_gate_a, b_gate_a, w_gate_x, b_gate_x,
           lru_lambda, w_out_rec, w_qkv, w_o, w_router, b_router, w_gu, b_gu, w_down, b_down):
    batch, seq, d = x.shape
    t = batch * seq
    bf16 = jnp.bfloat16
    x2 = x.reshape(t, d)
    cos2, sin2 = _rope_tables(seq, d // N_HEADS)
    for i in range(DEPTH):
        j = i // 2
        if i % 2 == 0:
            w = conv_w.shape[2]
            proj_tm = _lru_in_proj(x2, w_in_rec[j].astype(bf16), batch, seq)
            wg = jnp.concatenate([w_gate_a[j], w_gate_x[j]], axis=-1).astype(bf16)
            y = _lru(proj_tm, conv_w[j], conv_b[j], wg, b_gate_a[j], b_gate_x[j], lru_lambda[j], seq, batch)
            y = y.reshape(seq, batch * w)
            y_spec = pl.BlockSpec((ROW_TILE, w), lambda b, s: (s, b))
            w_out = w_out_rec[j]
        else:
            qkv = _qkv_proj(x2, w_qkv[j].astype(bf16), cos2, sin2, batch, seq)
            y = _moba(qkv, batch, seq, d).reshape(t, d)
            ns = seq // ROW_TILE
            y_spec = pl.BlockSpec((ROW_TILE, d), lambda b, s: (b * ns + s, 0))
            w_out = w_o[j]
        xn, idx, gates = _post_mixer(y, y_spec, x2, w_out.astype(bf16), ln1_g[i], ln1_b[i],
                                     w_router[i], b_router[i], batch, seq)
        blk_expert, tables, n_dump = _routing_tables(idx, t)
        y_scat = _moe_experts(xn, blk_expert, tables, n_dump, w_gu[i].astype(bf16), b_gu[i],
                              w_down[i].astype(bf16), b_down[i])
        x2 = _combine(y_scat, gates.T, xn, ln2_g[i], ln2_b[i])
    return x2.reshape(batch, seq, d)
```
